```python
import jax, jax.numpy as jnp
from jax import lax
import numpy as np

D_MODEL = 2048
BATCH = 4
SEQ = 2048
DEPTH = 1
DEC_BATCH = 8
DEC_SEQ = 1
PAST_LEN = 16384
PAGE_SIZE = 128

D_MIX = D_MODEL
D_RET = D_MIX // 2
D_ATT = D_MIX - D_RET
HEAD_DIM = 128
N_RET_HEADS = D_RET // HEAD_DIM
N_ATT_HEADS = D_ATT // HEAD_DIM
N_KV_HEADS = 2
N_GROUP = N_ATT_HEADS // N_KV_HEADS
D_KV = N_KV_HEADS * HEAD_DIM
N_IDX_HEADS = 16
D_IDX = 128
TOPK_MAX = 256
RET_CHUNK = 128
Q_BLOCK = 128
ROPE_BASE = 10000.0
NORM_EPS = 1e-6
SPLITS = (D_RET, D_RET, D_RET, D_RET, D_ATT, D_KV, D_KV, D_ATT, N_IDX_HEADS * D_IDX, D_IDX, N_IDX_HEADS)
D_IN_PROJ = 4 * D_RET + 2 * D_ATT + 2 * D_KV + N_IDX_HEADS * D_IDX + D_IDX + N_IDX_HEADS

kernel_name = "hymba_retention_dsa_step"


def rmsnorm(x, g):
    xf = x.astype(jnp.float32)
    y = xf * lax.rsqrt(jnp.mean(xf * xf, -1, keepdims=True) + NORM_EPS)
    return (y * g.astype(jnp.float32)).astype(x.dtype)


def head_group_norm(o, g):
    of = o.astype(jnp.float32)
    mu = jnp.mean(of, -1, keepdims=True)
    var = jnp.mean(jnp.square(of - mu), -1, keepdims=True)
    return ((of - mu) * lax.rsqrt(var + NORM_EPS) * g.astype(jnp.float32)).astype(o.dtype)


def rotary(x, pos):
    half = x.shape[-1] // 2
    inv = ROPE_BASE ** (-jnp.arange(half, dtype=jnp.float32) / half)
    ang = pos.astype(jnp.float32)[:, None] * inv[None, :]
    cos = jnp.cos(ang)[None, :, None, :].astype(x.dtype)
    sin = jnp.sin(ang)[None, :, None, :].astype(x.dtype)
    x1, x2 = x[..., :half], x[..., half:]
    return jnp.concatenate([x1 * cos - x2 * sin, x1 * sin + x2 * cos], -1)


def retention_log_decay():
    return jnp.log1p(-jnp.exp2(-5.0 - jnp.arange(N_RET_HEADS, dtype=jnp.float32)))


def retention_chunk(q, k, v, s_prev, log_g):
    c = q.shape[1]
    i = jnp.arange(c, dtype=jnp.float32)
    diff = i[:, None] - i[None, :]
    decay = jnp.where(diff[None] >= 0, jnp.exp(jnp.maximum(diff, 0.0)[None] * log_g[:, None, None]), 0.0).astype(q.dtype)
    scores = jnp.einsum('bihd,bjhd->bhij', q, k) * decay[None]
    inner = jnp.einsum('bhij,bjhe->bihe', scores, v)
    q_dec = jnp.exp((i[:, None] + 1.0) * log_g[None, :]).astype(q.dtype)
    cross = jnp.einsum('bihd,bhde->bihe', q, s_prev.astype(q.dtype)) * q_dec[None, :, :, None]
    k_dec = jnp.exp((c - 1.0 - i)[:, None] * log_g[None, :]).astype(k.dtype)
    s_new = jnp.exp(c * log_g).astype(s_prev.dtype)[None, :, None, None] * s_prev + jnp.einsum('bjhd,bjhe->bhde', k * k_dec[None, :, :, None], v)
    return inner + cross, s_new.astype(s_prev.dtype)


def retention_prompt(q, k, v, log_g):
    b, t, h, d = q.shape
    c = min(RET_CHUNK, t)
    nc = t // c
    def chunks(a):
        return jnp.moveaxis(a.reshape((b, nc, c) + a.shape[2:]), 1, 0)
    s0 = jnp.zeros((b, h, d, v.shape[-1]), dtype=v.dtype)
    def step(s, qkv):
        o, s = retention_chunk(qkv[0], qkv[1], qkv[2], s, log_g)
        return s, o
    s_fin, outs = lax.scan(step, s0, (chunks(q), chunks(k), chunks(v)))
    return jnp.moveaxis(outs, 0, 1).reshape(b, t, h, v.shape[-1]), s_fin


def project(x, g_norm, w_in, pos):
    b, t = x.shape[:2]
    z = rmsnorm(x, g_norm) @ w_in
    offs = [int(o) for o in np.cumsum(SPLITS)[:-1]]
    q_r, k_r, v_r, g_r, q_a, k_a, v_a, g_a, q_i, k_i, w_i = jnp.split(z, offs, axis=-1)
    q_r = rotary(q_r.reshape(b, t, N_RET_HEADS, HEAD_DIM), pos)
    k_r = rotary(k_r.reshape(b, t, N_RET_HEADS, HEAD_DIM), pos) * (HEAD_DIM ** -0.5)
    v_r = v_r.reshape(b, t, N_RET_HEADS, HEAD_DIM)
    q_a = q_a.reshape(b, t, N_KV_HEADS, N_GROUP, HEAD_DIM)
    k_a = k_a.reshape(b, t, N_KV_HEADS, HEAD_DIM)
    v_a = v_a.reshape(b, t, N_KV_HEADS, HEAD_DIM)
    q_i = q_i.reshape(b, t, N_IDX_HEADS, D_IDX)
    w_i = w_i * (N_IDX_HEADS ** -0.5 * D_IDX ** -0.5)
    return q_r, k_r, v_r, g_r, q_a, k_a, v_a, g_a, q_i, k_i, w_i


def indexer_select(q_i, w_i, k_i_all, q_pos, n_sel):
    rel = jax.nn.relu(jnp.einsum('bqhd,bsd->bqhs', q_i, k_i_all))
    score = jnp.einsum('bqh,bqhs->bqs', w_i, rel).astype(jnp.float32)
    key_pos = jnp.arange(k_i_all.shape[1], dtype=jnp.int32)
    score = jnp.where(key_pos[None, None, :] <= q_pos[None, :, None], score, -jnp.inf)
    _, idx = lax.top_k(score, n_sel)
    valid = idx <= q_pos[None, :, None]
    return idx, valid


def sparse_attend(q, k_g, v_g, valid):
    b, nq = q.shape[:2]
    logits = jnp.einsum('bqcgd,bqkcd->bqcgk', q, k_g).astype(jnp.float32) * (HEAD_DIM ** -0.5)
    logits = jnp.where(valid[:, :, None, None, :], logits, -jnp.inf)
    p = jax.nn.softmax(logits, axis=-1).astype(v_g.dtype)
    o = jnp.einsum('bqcgk,bqkcd->bqcgd', p, v_g)
    return o.reshape(b, nq, D_ATT)


def dsa_prompt(q_a, k_a, v_a, q_i, k_i, w_i):
    b, t = q_a.shape[:2]
    n_sel = min(TOPK_MAX, t // 4)
    qb = min(Q_BLOCK, t)
    nb = t // qb
    def blocks(a):
        return jnp.moveaxis(a.reshape((b, nb, qb) + a.shape[2:]), 1, 0)
    starts = jnp.arange(nb, dtype=jnp.int32) * qb
    gather_rows = jax.vmap(lambda rows, ids: rows[ids])
    def one_block(args):
        qa, qi, wi, st = args
        q_pos = st + jnp.arange(qb, dtype=jnp.int32)
        idx, valid = indexer_select(qi, wi, k_i, q_pos, n_sel)
        return sparse_attend(qa, gather_rows(k_a, idx), gather_rows(v_a, idx), valid)
    out = lax.map(one_block, (blocks(q_a), blocks(q_i), blocks(w_i), starts))
    return jnp.moveaxis(out, 0, 1).reshape(b, t, D_ATT)


def dsa_sample(q_a, k_a, v_a, q_i, k_i, w_i, pool_k, pool_v, pool_ki, page_table):
    db, ds = q_a.shape[:2]
    page = pool_k.shape[1]
    past = page_table.shape[1] * page
    n_sel = min(TOPK_MAX, (past + ds) // 4)
    ki_past = pool_ki[page_table].reshape(db, past, D_IDX)
    ki_all = jnp.concatenate([ki_past.astype(k_i.dtype), k_i], axis=1)
    q_pos = past + jnp.arange(ds, dtype=jnp.int32)
    idx, valid = indexer_select(q_i, w_i, ki_all, q_pos, n_sel)
    b_ix = jnp.arange(db)[:, None, None]
    in_past = (idx < past)[..., None, None]
    pidx = jnp.minimum(idx, past - 1)
    phys = page_table[b_ix, pidx // page]
    off = pidx % page
    nidx = jnp.clip(idx - past, 0, ds - 1)
    k_g = jnp.where(in_past, pool_k[phys, off].astype(k_a.dtype), k_a[b_ix, nidx])
    v_g = jnp.where(in_past, pool_v[phys, off].astype(v_a.dtype), v_a[b_ix, nidx])
    return sparse_attend(q_a, k_g, v_g, valid)


def merge_heads(o_r, g_r, o_a, g_a, ret_g, w_out):
    b, t = o_r.shape[:2]
    r = head_group_norm(o_r, ret_g).reshape(b, t, D_RET) * jax.nn.silu(g_r)
    a = o_a * jax.nn.silu(g_a)
    return jnp.concatenate([r, a], axis=-1) @ w_out


def setup_inputs(seed: int = 0) -> dict:
    key = jax.random.key(seed)
    ks = jax.random.split(key, 12)
    n_pages = PAST_LEN // PAGE_SIZE
    n_used = DEC_BATCH * n_pages
    n_phys = n_used + (n_used + 3) // 4
    f32 = jnp.float32
    x_prompt = jax.random.normal(ks[0], (BATCH, SEQ, D_MODEL), f32)
    x_sample = jax.random.normal(ks[1], (DEC_BATCH, DEC_SEQ, D_MODEL), f32)
    cache_k = jax.random.normal(ks[2], (DEPTH, n_phys, PAGE_SIZE, N_KV_HEADS, HEAD_DIM), f32)
    cache_v = jax.random.normal(ks[3], (DEPTH, n_phys, PAGE_SIZE, N_KV_HEADS, HEAD_DIM), f32)
    cache_kidx = jax.random.normal(ks[4], (DEPTH, n_phys, PAGE_SIZE, D_IDX), f32)
    state_ret = jax.random.normal(ks[5], (DEPTH, DEC_BATCH, N_RET_HEADS, HEAD_DIM, HEAD_DIM), f32)
    page_table = jax.random.permutation(ks[6], n_phys)[:n_used].reshape(DEC_BATCH, n_pages).astype(jnp.int32)
    norm_g = 1.0 + 0.02 * jax.random.normal(ks[7], (DEPTH, D_MODEL), f32)
    w_in = jax.random.normal(ks[8], (DEPTH, D_MODEL, D_IN_PROJ), f32) * (D_MODEL ** -0.5)
    ret_norm_g = 1.0 + 0.02 * jax.random.normal(ks[9], (DEPTH, N_RET_HEADS, HEAD_DIM), f32)
    w_out = jax.random.normal(ks[10], (DEPTH, D_MIX, D_MODEL), f32) * (D_MIX ** -0.5)
    final_norm_g = 1.0 + 0.02 * jax.random.normal(ks[11], (D_MODEL,), f32)
    return {"x_prompt": x_prompt, "x_sample": x_sample, "cache_k": cache_k, "cache_v": cache_v,
            "cache_kidx": cache_kidx, "state_ret": state_ret, "page_table": page_table,
            "norm_g": norm_g, "w_in": w_in, "ret_norm_g": ret_norm_g, "w_out": w_out,
            "final_norm_g": final_norm_g}


def reference(x_prompt, x_sample, cache_k, cache_v, cache_kidx, state_ret, page_table,
              norm_g, w_in, ret_norm_g, w_out, final_norm_g):
    log_g = retention_log_decay()
    t = x_prompt.shape[1]
    ds = x_sample.shape[1]
    past = page_table.shape[1] * cache_k.shape[2]
    pos_p = jnp.arange(t, dtype=jnp.int32)
    pos_s = past + jnp.arange(ds, dtype=jnp.int32)
    hp, hs = x_prompt, x_sample
    pk, pv, pki, pst = [], [], [], []
    sk, sv, ski, sst = [], [], [], []
    for l in range(DEPTH):
        q_r, k_r, v_r, g_r, q_a, k_a, v_a, g_a, q_i, k_i, w_i = project(hp, norm_g[l], w_in[l], pos_p)
        o_r, s_fin = retention_prompt(q_r, k_r, v_r, log_g)
        o_a = dsa_prompt(q_a, k_a, v_a, q_i, k_i, w_i)
        hp = hp + merge_heads(o_r, g_r, o_a, g_a, ret_norm_g[l], w_out[l])
        pk.append(k_a); pv.append(v_a); pki.append(k_i); pst.append(s_fin)
        q_r, k_r, v_r, g_r, q_a, k_a, v_a, g_a, q_i, k_i, w_i = project(hs, norm_g[l], w_in[l], pos_s)
        o_r, s_new = retention_chunk(q_r, k_r, v_r, state_ret[l], log_g)
        o_a = dsa_sample(q_a, k_a, v_a, q_i, k_i, w_i, cache_k[l], cache_v[l], cache_kidx[l], page_table)
        hs = hs + merge_heads(o_r, g_r, o_a, g_a, ret_norm_g[l], w_out[l])
        sk.append(k_a); sv.append(v_a); ski.append(k_i); sst.append(s_new)
    y_prompt = rmsnorm(hp, final_norm_g)
    y_sample = rmsnorm(hs, final_norm_g)
    return (y_prompt, y_sample, jnp.stack(pk), jnp.stack(pv), jnp.stack(pki), jnp.stack(pst),
            jnp.stack(sk), jnp.stack(sv), jnp.stack(ski), jnp.stack(sst))
```

```python
import functools

import numpy as np
import jax
import jax.numpy as jnp
from jax import lax
from jax.experimental import pallas as pl
from jax.experimental.pallas import tpu as pltpu

F32 = jnp.float32
BF16 = jnp.bfloat16
I32 = jnp.int32

HEAD_DIM = 128
N_RET_HEADS = 8
N_ATT_HEADS = 8
N_KV_HEADS = 2
N_GROUP = N_ATT_HEADS // N_KV_HEADS
N_IDX_HEADS = 16
D_IDX = 128
TOPK_MAX = 256
RET_CHUNK = 128
Q_BLOCK = 128
ROPE_BASE = 10000.0
NORM_EPS = 1e-6

D_RET = N_RET_HEADS * HEAD_DIM
D_ATT = N_ATT_HEADS * HEAD_DIM
D_KV = N_KV_HEADS * HEAD_DIM
OFF_QR = 0
OFF_KR = OFF_QR + D_RET
OFF_VR = OFF_KR + D_RET
OFF_GR = OFF_VR + D_RET
OFF_QA = OFF_GR + D_RET
OFF_KA = OFF_QA + D_ATT
OFF_VA = OFF_KA + D_KV
OFF_GA = OFF_VA + D_KV
OFF_QI = OFF_GA + D_ATT
OFF_KI = OFF_QI + N_IDX_HEADS * D_IDX
OFF_WI = OFF_KI + D_IDX
D_IN_PROJ = OFF_WI + N_IDX_HEADS

KEY_CHUNK = 256
INT_MIN = -(2 ** 31)
KEY_LOWEST_FINITE = INT_MIN + 2 ** 23
NEG_BIG = -1e30
V7X_VMEM_LIMIT = 50 * 1024 * 1024


def _cparams(sem):
    return pltpu.CompilerParams(dimension_semantics=sem, vmem_limit_bytes=V7X_VMEM_LIMIT)


def _silu(x):
    return x * (1.0 / (1.0 + jnp.exp(-x)))


def _key_to_float(key):
    return lax.bitcast_convert_type(jnp.where(key >= 0, key, key ^ jnp.int32(0x7FFFFFFF)), F32)


def _bisect_threshold(count_ge, shape, n_sel):
    def body(i, key):
        cand = key + lax.shift_left(jnp.int32(1), 31 - i)
        return jnp.where(count_ge(_key_to_float(cand)) >= float(n_sel), cand, key)

    key = lax.fori_loop(0, 32, body, jnp.full(shape, INT_MIN, I32))
    return _key_to_float(jnp.maximum(key, jnp.int32(KEY_LOWEST_FINITE)))


def _rmsnorm_kernel(x_ref, g_ref, o_ref):
    x = x_ref[...]
    ms = jnp.mean(x * x, axis=-1, keepdims=True)
    o_ref[...] = (x * lax.rsqrt(ms + NORM_EPS) * g_ref[...]).astype(o_ref.dtype)


def _rmsnorm_bf16(x2d, g, tm):
    n, d = x2d.shape
    return pl.pallas_call(
        _rmsnorm_kernel,
        grid=(n // tm,),
        in_specs=[pl.BlockSpec((tm, d), lambda i: (i, 0)),
                  pl.BlockSpec((1, d), lambda i: (0, 0))],
        out_specs=pl.BlockSpec((tm, d), lambda i: (i, 0)),
        out_shape=jax.ShapeDtypeStruct((n, d), BF16),
        compiler_params=_cparams(("parallel",)),
        name="rmsnorm_bf16",
    )(x2d, g.reshape(1, d))


def _inproj_kernel(h_ref, w_ref, hs_ref, z_ref, zs_ref, wb_ref):
    @pl.when(pl.program_id(1) == 0)
    def _():
        wb_ref[...] = w_ref[...].astype(BF16)
        zs_ref[...] = jnp.dot(hs_ref[...], wb_ref[...], preferred_element_type=F32)

    z_ref[...] = jnp.dot(h_ref[...], wb_ref[...], preferred_element_type=F32)


def _in_projection(h, hs, w_in, tm=1024, tn=1024):
    n, d = h.shape
    ns = hs.shape[0]
    dout = w_in.shape[1]
    return pl.pallas_call(
        _inproj_kernel,
        grid=(pl.cdiv(dout, tn), n // tm),
        in_specs=[pl.BlockSpec((tm, d), lambda j, i: (i, 0)),
                  pl.BlockSpec((d, tn), lambda j, i: (0, j)),
                  pl.BlockSpec((ns, d), lambda j, i: (0, 0))],
        out_specs=[pl.BlockSpec((tm, tn), lambda j, i: (i, j)),
                   pl.BlockSpec((ns, tn), lambda j, i: (0, j))],
        out_shape=[jax.ShapeDtypeStruct((n, dout), F32),
                   jax.ShapeDtypeStruct((ns, dout), F32)],
        scratch_shapes=[pltpu.VMEM((d, tn), BF16)],
        compiler_params=_cparams(("arbitrary", "arbitrary")),
        name="in_projection",
    )(h, w_in, hs)


def _rotate(x, c, s):
    return x * c + pltpu.roll(x, HEAD_DIM // 2, axis=1) * s


def _group_norm_gate(o, g_norm, gate):
    mu = jnp.mean(o, axis=-1, keepdims=True)
    d = o - mu
    var = jnp.mean(d * d, axis=-1, keepdims=True)
    return d * lax.rsqrt(var + NORM_EPS) * g_norm * _silu(gate)


def _retention_kernel(q_ref, k_ref, v_ref, g_ref, cos_ref, sin_ref, decay_ref, qdec_ref, kdec_ref,
                      gn_ref, gstate_ref, r_ref, s_ref):
    @pl.when(pl.program_id(1) == 0)
    def _():
        s_ref[...] = jnp.zeros_like(s_ref)

    c = cos_ref[...]
    s = sin_ref[...]
    for h in range(N_RET_HEADS):
        sl = slice(h * HEAD_DIM, (h + 1) * HEAD_DIM)
        q = _rotate(q_ref[:, sl], c, s)
        k = _rotate(k_ref[:, sl], c, s) * (HEAD_DIM ** -0.5)
        qb = q.astype(BF16)
        kb = k.astype(BF16)
        vb = v_ref[:, sl].astype(BF16)
        s_prev = s_ref[h]
        scores = lax.dot_general(qb, kb, (((1,), (1,)), ((), ())), preferred_element_type=F32)
        scores = scores * decay_ref[h]
        inner = jnp.dot(scores.astype(BF16), vb, preferred_element_type=F32)
        cross = jnp.dot(qb, s_prev.astype(BF16), preferred_element_type=F32) * qdec_ref[:, sl]
        kd_t = (k * kdec_ref[:, sl]).T.astype(BF16)
        s_ref[h] = gstate_ref[h] * s_prev + jnp.dot(kd_t, vb, preferred_element_type=F32)
        r_ref[:, sl] = _group_norm_gate(inner + cross, gn_ref[:, sl], g_ref[:, sl]).astype(r_ref.dtype)


def _retention_prompt(z, batch, seq, tabs, ret_g):
    nc = seq // RET_CHUNK
    wblk = D_RET

    def zspec(off):
        return pl.BlockSpec((RET_CHUNK, wblk), lambda b, c, o=off // wblk: (b * nc + c, o))

    full3 = pl.BlockSpec((N_RET_HEADS, HEAD_DIM, HEAD_DIM), lambda b, c: (0, 0, 0))
    row = pl.BlockSpec((RET_CHUNK, wblk), lambda b, c: (0, 0))
    return pl.pallas_call(
        _retention_kernel,
        grid=(batch, nc),
        in_specs=[zspec(OFF_QR), zspec(OFF_KR), zspec(OFF_VR), zspec(OFF_GR),
                  pl.BlockSpec((RET_CHUNK, HEAD_DIM), lambda b, c: (c, 0)),
                  pl.BlockSpec((RET_CHUNK, HEAD_DIM), lambda b, c: (c, 0)),
                  full3, row, row,
                  pl.BlockSpec((1, wblk), lambda b, c: (0, 0)),
                  full3],
        out_specs=[pl.BlockSpec((RET_CHUNK, wblk), lambda b, c: (b * nc + c, 0)),
                   pl.BlockSpec((None, N_RET_HEADS, HEAD_DIM, HEAD_DIM), lambda b, c: (b, 0, 0, 0))],
        out_shape=[jax.ShapeDtypeStruct((batch * seq, wblk), BF16),
                   jax.ShapeDtypeStruct((batch, N_RET_HEADS, HEAD_DIM, HEAD_DIM), F32)],
        compiler_params=_cparams(("parallel", "arbitrary")),
        name="retention_prompt",
    )(z, z, z, z, tabs["cos"], tabs["sin"], tabs["decay"], tabs["qdec"], tabs["kdec"],
      ret_g.reshape(1, wblk), tabs["gstate"])


def _dsa_prompt_kernel(qa_ref, ga0_ref, ga1_ref, qi0_ref, qi1_ref, qi2_ref, qi3_ref, wi_ref,
                       ki_ref, ka_ref, va_ref, a_ref,
                       kib, kab, vab, qall, qc, sc, m_ref, l_ref, acc_ref, *, n_sel):
    j = pl.program_id(1)
    nkc_all = ki_ref.shape[0] // KEY_CHUNK

    @pl.when(j == 0)
    def _():
        for kc in range(nkc_all):
            rows = slice(kc * KEY_CHUNK, (kc + 1) * KEY_CHUNK)
            kib[kc] = ki_ref[rows, :].astype(BF16)
            for c in range(N_KV_HEADS):
                cols = slice(c * HEAD_DIM, (c + 1) * HEAD_DIM)
                kab[c, kc] = ka_ref[rows, cols].astype(BF16)
                vab[c, kc] = va_ref[rows, cols].astype(BF16)

    nch = (j * Q_BLOCK + Q_BLOCK + KEY_CHUNK - 1) // KEY_CHUNK

    qi_refs = (qi0_ref, qi1_ref, qi2_ref, qi3_ref)
    per = N_IDX_HEADS // len(qi_refs)
    for h in range(N_IDX_HEADS):
        src = qi_refs[h // per]
        qall[h * Q_BLOCK:(h + 1) * Q_BLOCK, :] = src[:, (h % per) * D_IDX:(h % per + 1) * D_IDX].astype(BF16)
    scale = HEAD_DIM ** -0.5
    for c in range(N_KV_HEADS):
        for g in range(N_GROUP):
            hh = c * N_GROUP + g
            qc[c, g * Q_BLOCK:(g + 1) * Q_BLOCK, :] = (
                qa_ref[:, hh * HEAD_DIM:(hh + 1) * HEAD_DIM] * scale).astype(BF16)

    w_norm = N_IDX_HEADS ** -0.5 * D_IDX ** -0.5
    wcols = [wi_ref[:, h:h + 1] * w_norm for h in range(N_IDX_HEADS)]
    t_pos = j * Q_BLOCK + lax.broadcasted_iota(I32, (Q_BLOCK, KEY_CHUNK), 0)
    s_loc = lax.broadcasted_iota(I32, (Q_BLOCK, KEY_CHUNK), 1)

    def idx_body(kc, carry):
        res = lax.dot_general(qall[...], kib[kc], (((1,), (1,)), ((), ())), preferred_element_type=F32)
        acc = jnp.zeros((Q_BLOCK, KEY_CHUNK), F32)
        for h in range(N_IDX_HEADS):
            acc = acc + wcols[h] * jnp.maximum(res[h * Q_BLOCK:(h + 1) * Q_BLOCK, :], 0.0)
        causal = (kc * KEY_CHUNK + s_loc) <= t_pos
        sc[kc] = jnp.where(causal, acc, -jnp.inf)
        return carry

    lax.fori_loop(0, nch, idx_body, 0)

    def count_ge(cand):
        def cnt_body(kc, cnt):
            return cnt + jnp.where(sc[kc] >= cand, 1.0, 0.0)

        cnt = lax.fori_loop(0, nch, cnt_body, jnp.zeros((Q_BLOCK, KEY_CHUNK), F32))
        return jnp.sum(cnt, axis=1, keepdims=True)

    thr = _bisect_threshold(count_ge, (Q_BLOCK, 1), n_sel)

    gates = (ga0_ref, ga1_ref)
    for c in range(N_KV_HEADS):
        m_ref[...] = jnp.full(m_ref.shape, NEG_BIG, F32)
        l_ref[...] = jnp.zeros(l_ref.shape, F32)
        acc_ref[...] = jnp.zeros(acc_ref.shape, F32)

        def att_body(kc, carry, c=c):
            lg = lax.dot_general(qc[c], kab[c, kc], (((1,), (1,)), ((), ())), preferred_element_type=F32)
            bias = jnp.where(sc[kc] >= thr, 0.0, NEG_BIG)
            vch = vab[c, kc]
            for g in range(N_GROUP):
                rows = slice(g * Q_BLOCK, (g + 1) * Q_BLOCK)
                lgg = lg[rows, :] + bias
                m_prev = m_ref[rows, :]
                m_new = jnp.maximum(m_prev, jnp.max(lgg, axis=1, keepdims=True))
                alpha = jnp.exp(m_prev - m_new)
                p = jnp.exp(lgg - m_new[:, :1])
                l_ref[rows, :] = alpha * l_ref[rows, :] + jnp.sum(p, axis=1, keepdims=True)
                acc_ref[rows, :] = alpha * acc_ref[rows, :] + jnp.dot(
                    p.astype(BF16), vch, preferred_element_type=F32)
                m_ref[rows, :] = m_new
            return carry

        lax.fori_loop(0, nch, att_body, 0)

        for g in range(N_GROUP):
            rows = slice(g * Q_BLOCK, (g + 1) * Q_BLOCK)
            o = acc_ref[rows, :] / l_ref[rows, :]
            gate = gates[c][:, g * HEAD_DIM:(g + 1) * HEAD_DIM]
            hh = c * N_GROUP + g
            a_ref[:, hh * HEAD_DIM:(hh + 1) * HEAD_DIM] = (o * _silu(gate)).astype(a_ref.dtype)


def _dsa_prompt(z, batch, seq):
    nq = seq // Q_BLOCK
    n_sel = min(TOPK_MAX, seq // 4)
    nkc = seq // KEY_CHUNK

    def qspec(off, width):
        return pl.BlockSpec((Q_BLOCK, width), lambda b, j, o=off // width: (b * nq + j, o))

    def kspec(off, width):
        return pl.BlockSpec((seq, width), lambda b, j, o=off // width: (b, o))

    half = D_ATT // 2
    quarter = N_IDX_HEADS * D_IDX // 4
    kernel = functools.partial(_dsa_prompt_kernel, n_sel=n_sel)
    return pl.pallas_call(
        kernel,
        grid=(batch, nq),
        in_specs=[qspec(OFF_QA, D_ATT),
                  qspec(OFF_GA, half), qspec(OFF_GA + half, half),
                  qspec(OFF_QI, quarter), qspec(OFF_QI + quarter, quarter),
                  qspec(OFF_QI + 2 * quarter, quarter), qspec(OFF_QI + 3 * quarter, quarter),
                  qspec(OFF_WI, 128),
                  kspec(OFF_KI, D_IDX), kspec(OFF_KA, D_KV), kspec(OFF_VA, D_KV)],
        out_specs=pl.BlockSpec((Q_BLOCK, D_ATT), lambda b, j: (b * nq + j, 0)),
        out_shape=jax.ShapeDtypeStruct((batch * seq, D_ATT), BF16),
        scratch_shapes=[pltpu.VMEM((nkc, KEY_CHUNK, D_IDX), BF16),
                        pltpu.VMEM((N_KV_HEADS, nkc, KEY_CHUNK, HEAD_DIM), BF16),
                        pltpu.VMEM((N_KV_HEADS, nkc, KEY_CHUNK, HEAD_DIM), BF16),
                        pltpu.VMEM((N_IDX_HEADS * Q_BLOCK, D_IDX), BF16),
                        pltpu.VMEM((N_KV_HEADS, N_GROUP * Q_BLOCK, HEAD_DIM), BF16),
                        pltpu.VMEM((nkc, Q_BLOCK, KEY_CHUNK), F32),
                        pltpu.VMEM((N_GROUP * Q_BLOCK, HEAD_DIM), F32),
                        pltpu.VMEM((N_GROUP * Q_BLOCK, HEAD_DIM), F32),
                        pltpu.VMEM((N_GROUP * Q_BLOCK, HEAD_DIM), F32)],
        compiler_params=_cparams(("parallel", "arbitrary")),
        name="dsa_prompt",
    )(z, z, z, z, z, z, z, z, z, z, z)


def _outproj_kernel(r_ref, a_ref, x_ref, w_ref, g_ref, y_ref):
    m = jnp.dot(r_ref[...], w_ref[:D_RET, :], preferred_element_type=F32)
    m = m + jnp.dot(a_ref[...], w_ref[D_RET:, :], preferred_element_type=F32)
    hp = x_ref[...] + m
    ms = jnp.mean(hp * hp, axis=-1, keepdims=True)
    y_ref[...] = hp * lax.rsqrt(ms + NORM_EPS) * g_ref[...]


def _out_projection(r, a, x2d, w_out_bf16, final_g, tm=512):
    n, d = x2d.shape
    return pl.pallas_call(
        _outproj_kernel,
        grid=(n // tm,),
        in_specs=[pl.BlockSpec((tm, D_RET), lambda i: (i, 0)),
                  pl.BlockSpec((tm, D_ATT), lambda i: (i, 0)),
                  pl.BlockSpec((tm, d), lambda i: (i, 0)),
                  pl.BlockSpec((D_RET + D_ATT, d), lambda i: (0, 0)),
                  pl.BlockSpec((1, d), lambda i: (0, 0))],
        out_specs=pl.BlockSpec((tm, d), lambda i: (i, 0)),
        out_shape=jax.ShapeDtypeStruct((n, d), F32),
        compiler_params=_cparams(("parallel",)),
        name="out_projection",
    )(r, a, x2d, w_out_bf16, final_g.reshape(1, d))


PAGES_PER_STEP = 8


def _sample_scores_kernel(pt_ref, qi_ref, wi_ref, *refs):
    page_refs = refs[:PAGES_PER_STEP]
    out_ref = refs[PAGES_PER_STEP]
    qb = qi_ref[...].astype(BF16)
    w = wi_ref[...] * (N_IDX_HEADS ** -0.5 * D_IDX ** -0.5)
    rows = []
    for p in range(PAGES_PER_STEP):
        kp = page_refs[p][...].astype(BF16)
        rel = lax.dot_general(qb, kp, (((1,), (1,)), ((), ())), preferred_element_type=F32)
        rows.append(jnp.sum(w * jnp.maximum(rel, 0.0), axis=0, keepdims=True))
    out_ref[...] = jnp.concatenate(rows, axis=0)


def _sample_scores(qi_s, wi_s, pool_ki, page_table):
    db, n_pages = page_table.shape
    page = pool_ki.shape[1]
    groups = n_pages // PAGES_PER_STEP

    def pspec(p):
        return pl.BlockSpec((None, page, D_IDX), lambda b, g, pt, p=p: (pt[b, g * PAGES_PER_STEP + p], 0, 0))

    grid_spec = pltpu.PrefetchScalarGridSpec(
        num_scalar_prefetch=1,
        grid=(db, groups),
        in_specs=[pl.BlockSpec((None, N_IDX_HEADS, D_IDX), lambda b, g, pt: (b, 0, 0)),
                  pl.BlockSpec((None, N_IDX_HEADS, 1), lambda b, g, pt: (b, 0, 0))]
                 + [pspec(p) for p in range(PAGES_PER_STEP)],
        out_specs=pl.BlockSpec((None, PAGES_PER_STEP, page), lambda b, g, pt: (b, g, 0)),
    )
    return pl.pallas_call(
        _sample_scores_kernel,
        grid_spec=grid_spec,
        out_shape=jax.ShapeDtypeStruct((db, n_pages, page), F32),
        compiler_params=_cparams(("parallel", "arbitrary")),
        name="sample_scores",
    )(page_table, qi_s, wi_s, *([pool_ki] * PAGES_PER_STEP))


def _sample_threshold_kernel(sc_ref, qi_ref, wi_ref, kin_ref, thr_ref, snew_sc_ref, *, n_sel):
    prod = qi_ref[...] * jnp.concatenate([kin_ref[...]] * N_IDX_HEADS, axis=1)
    w = wi_ref[...] * (N_IDX_HEADS ** -0.5 * D_IDX ** -0.5)
    s_new = jnp.zeros((prod.shape[0], 1), F32)
    for h in range(N_IDX_HEADS):
        rel = jnp.sum(prod[:, h * D_IDX:(h + 1) * D_IDX], axis=1, keepdims=True)
        s_new = s_new + w[:, h:h + 1] * jnp.maximum(rel, 0.0)
    past_scores = sc_ref[...]

    def count_ge(cand):
        cnt = jnp.sum(jnp.where(past_scores >= cand, 1.0, 0.0), axis=1, keepdims=True)
        return cnt + jnp.where(s_new >= cand, 1.0, 0.0)

    thr_ref[...] = _bisect_threshold(count_ge, s_new.shape, n_sel)
    snew_sc_ref[...] = s_new


def _sample_threshold(scores2d, qi_s, wi_s, ki_new, n_sel):
    db = scores2d.shape[0]
    kernel = functools.partial(_sample_threshold_kernel, n_sel=n_sel)
    return pl.pallas_call(
        kernel,
        out_shape=[jax.ShapeDtypeStruct((db, 1), F32), jax.ShapeDtypeStruct((db, 1), F32)],
        compiler_params=pltpu.CompilerParams(vmem_limit_bytes=V7X_VMEM_LIMIT),
        name="sample_threshold",
    )(scores2d, qi_s, wi_s, ki_new)


def _sample_attend_kernel(pt_ref, sc_ref, thr_ref, snew_sc_ref, q_ref, kn_ref, vn_ref, ga_ref, *refs):
    kp_refs = refs[:PAGES_PER_STEP]
    vp_refs = refs[PAGES_PER_STEP:2 * PAGES_PER_STEP]
    o_ref = refs[2 * PAGES_PER_STEP]
    m_ref, l_ref, acc_ref = refs[2 * PAGES_PER_STEP + 1:]
    g = pl.program_id(1)

    @pl.when(g == 0)
    def _():
        m_ref[...] = jnp.full(m_ref.shape, NEG_BIG, F32)
        l_ref[...] = jnp.zeros(l_ref.shape, F32)
        acc_ref[...] = jnp.zeros(acc_ref.shape, F32)

    thr = thr_ref[...]
    q2 = q_ref[...]
    qb = q2.astype(BF16)
    kcat = jnp.concatenate([r[...] for r in kp_refs], axis=0).astype(BF16)
    vcat = jnp.concatenate([r[...] for r in vp_refs], axis=0).astype(BF16)
    lg = lax.dot_general(qb, kcat, (((1,), (1,)), ((), ())), preferred_element_type=F32)
    step_scores = jnp.concatenate([sc_ref[p:p + 1, :] for p in range(PAGES_PER_STEP)], axis=1)
    lg = lg + jnp.where(step_scores >= thr, 0.0, NEG_BIG)
    m_prev = m_ref[...]
    m_new = jnp.maximum(m_prev, jnp.max(lg, axis=1, keepdims=True))
    alpha = jnp.exp(m_prev - m_new)
    p = jnp.exp(lg - m_new[:, :1])
    l_ref[...] = alpha * l_ref[...] + jnp.sum(p, axis=1, keepdims=True)
    acc_ref[...] = jnp.concatenate([alpha, alpha], axis=1) * acc_ref[...] + jnp.dot(
        p.astype(BF16), vcat, preferred_element_type=F32)
    m_ref[...] = m_new

    @pl.when(g == pl.num_programs(1) - 1)
    def _():
        head = lax.broadcasted_iota(I32, (N_ATT_HEADS, D_KV), 0)
        col = lax.broadcasted_iota(I32, (N_ATT_HEADS, D_KV), 1)
        own = (col // HEAD_DIM) == (head // N_GROUP)
        lg_n = jnp.sum(q2 * kn_ref[...], axis=1, keepdims=True)
        lg_n = lg_n + jnp.where(snew_sc_ref[...] >= thr, 0.0, NEG_BIG)
        m_p = m_ref[...]
        m_n = jnp.maximum(m_p, lg_n)
        al = jnp.exp(m_p - m_n)
        p_n = jnp.exp(lg_n - m_n[:, :1])
        l_fin = al * l_ref[...] + p_n
        acc = jnp.concatenate([al, al], axis=1) * acc_ref[...] + p_n * vn_ref[...]
        o2 = jnp.where(own, acc / jnp.concatenate([l_fin, l_fin], axis=1), 0.0)
        o = o2[:, :HEAD_DIM] + o2[:, HEAD_DIM:]
        o_ref[...] = o * _silu(ga_ref[...])


def _sample_attend(scores, thr, score_new, q_blk, kn, vn, ga_s, pool_k, pool_v, page_table):
    db, n_pages = page_table.shape
    page = pool_k.shape[1]
    groups = n_pages // PAGES_PER_STEP

    def pspec(p):
        return pl.BlockSpec((None, page, D_KV), lambda b, g, pt, p=p: (pt[b, g * PAGES_PER_STEP + p], 0, 0))

    def bspec(shape):
        return pl.BlockSpec((None,) + shape, lambda b, g, pt: (b,) + (0,) * len(shape))

    grid_spec = pltpu.PrefetchScalarGridSpec(
        num_scalar_prefetch=1,
        grid=(db, groups),
        in_specs=[pl.BlockSpec((None, PAGES_PER_STEP, page), lambda b, g, pt: (b, g, 0)),
                  bspec((1, 1)), bspec((1, 1)),
                  bspec((N_ATT_HEADS, D_KV)), bspec((1, D_KV)), bspec((1, D_KV)),
                  bspec((N_ATT_HEADS, HEAD_DIM))]
                 + [pspec(p) for p in range(PAGES_PER_STEP)] * 2,
        out_specs=bspec((N_ATT_HEADS, HEAD_DIM)),
        scratch_shapes=[pltpu.VMEM((N_ATT_HEADS, HEAD_DIM), F32),
                        pltpu.VMEM((N_ATT_HEADS, HEAD_DIM), F32),
                        pltpu.VMEM((N_ATT_HEADS, D_KV), F32)],
    )
    return pl.pallas_call(
        _sample_attend_kernel,
        grid_spec=grid_spec,
        out_shape=jax.ShapeDtypeStruct((db, N_ATT_HEADS, HEAD_DIM), F32),
        compiler_params=_cparams(("parallel", "arbitrary")),
        name="sample_attend",
    )(page_table, scores, thr.reshape(db, 1, 1), score_new.reshape(db, 1, 1), q_blk, kn, vn, ga_s,
      *([pool_k] * PAGES_PER_STEP), *([pool_v] * PAGES_PER_STEP))


def _sample_retention_kernel(q_ref, k_ref, v_ref, g_ref, cos_ref, sin_ref, gn_ref, gam_ref,
                             st_ref, r_ref, snew_ref):
    c = cos_ref[...]
    s = sin_ref[...]
    q = _rotate(q_ref[...], c, s)
    k = _rotate(k_ref[...], c, s) * (HEAD_DIM ** -0.5)
    v = v_ref[...]
    inner = jnp.sum(q * k, axis=1, keepdims=True) * v
    k_t = jnp.concatenate([k, jnp.zeros((HEAD_DIM - N_RET_HEADS, HEAD_DIM), F32)], axis=0).T
    outs = []
    for h in range(N_RET_HEADS):
        s_prev = st_ref[h]
        gam = gam_ref[h]
        q8 = jnp.broadcast_to(q[h:h + 1, :], (8, HEAD_DIM)).astype(BF16)
        cross = jnp.dot(q8, s_prev.astype(BF16), preferred_element_type=F32)[:1, :] * gam[:1, :]
        outs.append(inner[h:h + 1, :] + cross)
        snew_ref[h] = gam * s_prev + k_t[:, h:h + 1] * v[h:h + 1, :]
    o = jnp.concatenate(outs, axis=0)
    r_ref[...] = _group_norm_gate(o, gn_ref[...], g_ref[...])


def _sample_retention(qr, kr, vr, gr, cos_s, sin_s, ret_g, gam_tab, state):
    db = qr.shape[0]

    def bspec(shape):
        return pl.BlockSpec((None,) + shape, lambda b: (b,) + (0,) * len(shape))

    def cspec(shape):
        return pl.BlockSpec(shape, lambda b: (0,) * len(shape))

    hd = (N_RET_HEADS, HEAD_DIM)
    st = (N_RET_HEADS, HEAD_DIM, HEAD_DIM)
    return pl.pallas_call(
        _sample_retention_kernel,
        grid=(db,),
        in_specs=[bspec(hd), bspec(hd), bspec(hd), bspec(hd),
                  cspec((1, HEAD_DIM)), cspec((1, HEAD_DIM)), cspec(hd), cspec(st), bspec(st)],
        out_specs=[bspec(hd), bspec(st)],
        out_shape=[jax.ShapeDtypeStruct((db,) + hd, F32), jax.ShapeDtypeStruct((db,) + st, F32)],
        compiler_params=_cparams(("parallel",)),
        name="sample_retention",
    )(qr, kr, vr, gr, cos_s, sin_s, ret_g, gam_tab, state)


def _sample_out_kernel(r_ref, a_ref, x_ref, w_ref, g_ref, y_ref):
    m = jnp.dot(r_ref[...].astype(BF16), w_ref[:D_RET, :], preferred_element_type=F32)
    m = m + jnp.dot(a_ref[...].astype(BF16), w_ref[D_RET:, :], preferred_element_type=F32)
    hp = x_ref[...] + m
    ms = jnp.mean(hp * hp, axis=-1, keepdims=True)
    y_ref[...] = hp * lax.rsqrt(ms + NORM_EPS) * g_ref[...]


def _sample_out(r_s, a_s, x_s, w_out_bf16, final_g):
    n, d = x_s.shape
    return pl.pallas_call(
        _sample_out_kernel,
        out_shape=jax.ShapeDtypeStruct((n, d), F32),
        compiler_params=pltpu.CompilerParams(vmem_limit_bytes=V7X_VMEM_LIMIT),
        name="sample_out",
    )(r_s, a_s, x_s, w_out_bf16, final_g.reshape(1, d))


def _rotary_tables(pos):
    half = HEAD_DIM // 2
    inv = ROPE_BASE ** (-jnp.arange(half, dtype=F32) / half)
    ang = pos.astype(F32)[:, None] * inv[None, :]
    cos, sin = jnp.cos(ang), jnp.sin(ang)
    return jnp.concatenate([cos, cos], -1), jnp.concatenate([-sin, sin], -1)


def _retention_tables(seq):
    log_g = jnp.log1p(-jnp.exp2(-5.0 - jnp.arange(N_RET_HEADS, dtype=F32)))
    c = RET_CHUNK
    i = jnp.arange(c, dtype=F32)
    diff = i[:, None] - i[None, :]
    decay = jnp.where(diff[None] >= 0, jnp.exp(jnp.maximum(diff, 0.0)[None] * log_g[:, None, None]), 0.0)
    q_dec = jnp.exp((i[:, None] + 1.0) * log_g[None, :])
    k_dec = jnp.exp((c - 1.0 - i)[:, None] * log_g[None, :])
    ones = jnp.ones((1, 1, HEAD_DIM), F32)
    cos, sin = _rotary_tables(jnp.arange(seq, dtype=I32))
    return {
        "decay": decay,
        "qdec": (q_dec[:, :, None] * ones).reshape(c, D_RET),
        "kdec": (k_dec[:, :, None] * ones).reshape(c, D_RET),
        "gstate": jnp.exp(c * log_g)[:, None, None] * jnp.ones((1, HEAD_DIM, HEAD_DIM), F32),
        "gam1": jnp.exp(log_g)[:, None, None] * jnp.ones((1, HEAD_DIM, HEAD_DIM), F32),
        "cos": cos, "sin": sin,
    }


def kernel(x_prompt, x_sample, cache_k, cache_v, cache_kidx, state_ret, page_table,
           norm_g, w_in, ret_norm_g, w_out, final_norm_g):
    depth = w_in.shape[0]
    batch, seq, d_model = x_prompt.shape
    db, ds, _ = x_sample.shape
    assert depth == 1 and ds == 1 and seq % KEY_CHUNK == 0 and w_in.shape[2] == D_IN_PROJ
    page = cache_k.shape[2]
    past = page_table.shape[1] * page
    n_sel_s = min(TOPK_MAX, (past + ds) // 4)

    tabs = _retention_tables(seq)
    cos_s, sin_s = _rotary_tables(past + jnp.arange(ds, dtype=I32))

    hp = x_prompt.reshape(batch * seq, d_model)
    hs = x_sample.reshape(db * ds, d_model)
    pk, pv, pki, pst, sk, sv, ski, sst = [], [], [], [], [], [], [], []
    for l in range(depth):
        h_b = _rmsnorm_bf16(hp, norm_g[l], 512)
        hs_b = _rmsnorm_bf16(hs, norm_g[l], db * ds)
        z, zs = _in_projection(h_b, hs_b, w_in[l])
        w_out_b = w_out[l].astype(BF16)
        fin_g = final_norm_g

        r, s_fin = _retention_prompt(z, batch, seq, tabs, ret_norm_g[l])
        a = _dsa_prompt(z, batch, seq)
        hp = _out_projection(r, a, hp, w_out_b, fin_g)
        pk.append(z[:, OFF_KA:OFF_VA].reshape(batch, seq, N_KV_HEADS, HEAD_DIM))
        pv.append(z[:, OFF_VA:OFF_GA].reshape(batch, seq, N_KV_HEADS, HEAD_DIM))
        pki.append(z[:, OFF_KI:OFF_WI].reshape(batch, seq, D_IDX))
        pst.append(s_fin)

        def seg(off, width):
            return zs[:, off:off + width]

        qi_s = seg(OFF_QI, N_IDX_HEADS * D_IDX).reshape(db, N_IDX_HEADS, D_IDX)
        wi_s = seg(OFF_WI, N_IDX_HEADS).reshape(db, N_IDX_HEADS, 1)
        ki_s = seg(OFF_KI, D_IDX)
        ka_s = seg(OFF_KA, D_KV)
        va_s = seg(OFF_VA, D_KV)
        pool_ki = cache_kidx[l]
        pool_k = cache_k[l].reshape(-1, page, D_KV)
        pool_v = cache_v[l].reshape(-1, page, D_KV)

        scores = _sample_scores(qi_s, wi_s, pool_ki, page_table)
        thr, score_new = _sample_threshold(scores.reshape(db, past), seg(OFF_QI, N_IDX_HEADS * D_IDX),
                                         seg(OFF_WI, N_IDX_HEADS), ki_s, n_sel_s)
        qa_s = seg(OFF_QA, D_ATT).reshape(db, N_KV_HEADS, N_GROUP, 1, HEAD_DIM) * (HEAD_DIM ** -0.5)
        eye = jnp.eye(N_KV_HEADS, dtype=F32)[None, :, None, :, None]
        q_blk = (qa_s * eye).reshape(db, N_ATT_HEADS, D_KV)
        a_s = _sample_attend(scores, thr, score_new, q_blk, ka_s.reshape(db, 1, D_KV),
                             va_s.reshape(db, 1, D_KV),
                             seg(OFF_GA, D_ATT).reshape(db, N_ATT_HEADS, HEAD_DIM),
                             pool_k, pool_v, page_table)

        hd = (db, N_RET_HEADS, HEAD_DIM)
        r_s, s_new = _sample_retention(seg(OFF_QR, D_RET).reshape(hd), seg(OFF_KR, D_RET).reshape(hd),
                                       seg(OFF_VR, D_RET).reshape(hd), seg(OFF_GR, D_RET).reshape(hd),
                                       cos_s, sin_s, ret_norm_g[l], tabs["gam1"], state_ret[l])
        hs = _sample_out(r_s.reshape(db, D_RET), a_s.reshape(db, D_ATT), hs, w_out_b, fin_g)
        sk.append(ka_s.reshape(db, ds, N_KV_HEADS, HEAD_DIM))
        sv.append(va_s.reshape(db, ds, N_KV_HEADS, HEAD_DIM))
        ski.append(ki_s.reshape(db, ds, D_IDX))
        sst.append(s_new)

    y_prompt = hp.reshape(batch, seq, d_model)
    y_sample = hs.reshape(db, ds, d_model)
    return (y_prompt, y_sample, jnp.stack(pk), jnp.stack(pv), jnp.stack(pki), jnp.stack(pst),
            jnp.stack(sk), jnp.stack(sv), jnp.stack(ski), jnp.stack(sst))
```

```python
import functools

import numpy as np
import jax
import jax.numpy as jnp
from jax import lax
from jax.experimental import pallas as pl
from jax.experimental.pallas import tpu as pltpu

F32 = jnp.float32
BF16 = jnp.bfloat16
I32 = jnp.int32

HEAD_DIM = 128
N_RET_HEADS = 8
N_ATT_HEADS = 8
N_KV_HEADS = 2
N_GROUP = N_ATT_HEADS // N_KV_HEADS
N_IDX_HEADS = 16
D_IDX = 128
TOPK_MAX = 256
RET_CHUNK = 128
Q_BLOCK = 128
ROPE_BASE = 10000.0
NORM_EPS = 1e-6

D_RET = N_RET_HEADS * HEAD_DIM
D_ATT = N_ATT_HEADS * HEAD_DIM
D_KV = N_KV_HEADS * HEAD_DIM
OFF_QR = 0
OFF_KR = OFF_QR + D_RET
OFF_VR = OFF_KR + D_RET
OFF_GR = OFF_VR + D_RET
OFF_QA = OFF_GR + D_RET
OFF_KA = OFF_QA + D_ATT
OFF_VA = OFF_KA + D_KV
OFF_GA = OFF_VA + D_KV
OFF_QI = OFF_GA + D_ATT
OFF_KI = OFF_QI + N_IDX_HEADS * D_IDX
OFF_WI = OFF_KI + D_IDX
D_IN_PROJ = OFF_WI + N_IDX_HEADS

KEY_CHUNK = 256
INT_MIN = -(2 ** 31)
KEY_LOWEST_FINITE = INT_MIN + 2 ** 23
NEG_BIG = -1e30
V7X_VMEM_LIMIT = 50 * 1024 * 1024


def _cparams(sem):
    return pltpu.CompilerParams(dimension_semantics=sem, vmem_limit_bytes=V7X_VMEM_LIMIT)


def _silu(x):
    return x * (1.0 / (1.0 + jnp.exp(-x)))


def _key_to_float(key):
    return lax.bitcast_convert_type(jnp.where(key >= 0, key, key ^ jnp.int32(0x7FFFFFFF)), F32)


def _bisect_threshold(count_ge, shape, n_sel):
    def body(i, key):
        cand = key + lax.shift_left(jnp.int32(1), 31 - i)
        return jnp.where(count_ge(_key_to_float(cand)) >= float(n_sel), cand, key)

    key = lax.fori_loop(0, 32, body, jnp.full(shape, INT_MIN, I32))
    return _key_to_float(jnp.maximum(key, jnp.int32(KEY_LOWEST_FINITE)))


def _rmsnorm_kernel(x_ref, g_ref, o_ref):
    x = x_ref[...]
    ms = jnp.mean(x * x, axis=-1, keepdims=True)
    o_ref[...] = (x * lax.rsqrt(ms + NORM_EPS) * g_ref[...]).astype(o_ref.dtype)


def _rmsnorm_bf16(x2d, g, tm):
    n, d = x2d.shape
    return pl.pallas_call(
        _rmsnorm_kernel,
        grid=(n // tm,),
        in_specs=[pl.BlockSpec((tm, d), lambda i: (i, 0)),
                  pl.BlockSpec((1, d), lambda i: (0, 0))],
        out_specs=pl.BlockSpec((tm, d), lambda i: (i, 0)),
        out_shape=jax.ShapeDtypeStruct((n, d), BF16),
        compiler_params=_cparams(("parallel",)),
        name="rmsnorm_bf16",
    )(x2d, g.reshape(1, d))


_NT = (((1,), (1,)), ((), ()))


def _inproj_kernel(h_ref, wt_ref, hs_ref, z_ref, zs_ref, wb_ref):
    @pl.when(pl.program_id(1) == 0)
    def _():
        wb_ref[...] = wt_ref[...].astype(BF16)
        zs_ref[...] = lax.dot_general(hs_ref[...], wb_ref[...], _NT, preferred_element_type=F32)

    z_ref[...] = lax.dot_general(h_ref[...], wb_ref[...], _NT, preferred_element_type=F32)


def _in_projection(h, hs, w_in_t, tm=1024, tn=1024):
    n, d = h.shape
    ns = hs.shape[0]
    dout = w_in_t.shape[0]
    return pl.pallas_call(
        _inproj_kernel,
        grid=(pl.cdiv(dout, tn), n // tm),
        in_specs=[pl.BlockSpec((tm, d), lambda j, i: (i, 0)),
                  pl.BlockSpec((tn, d), lambda j, i: (j, 0)),
                  pl.BlockSpec((ns, d), lambda j, i: (0, 0))],
        out_specs=[pl.BlockSpec((tm, tn), lambda j, i: (i, j)),
                   pl.BlockSpec((ns, tn), lambda j, i: (0, j))],
        out_shape=[jax.ShapeDtypeStruct((n, dout), F32),
                   jax.ShapeDtypeStruct((ns, dout), F32)],
        scratch_shapes=[pltpu.VMEM((tn, d), BF16)],
        compiler_params=_cparams(("arbitrary", "arbitrary")),
        name="in_projection",
    )(h, w_in_t, hs)


def _rotate(x, c, s):
    return x * c + pltpu.roll(x, HEAD_DIM // 2, axis=1) * s


def _group_norm_gate(o, g_norm, gate):
    mu = jnp.mean(o, axis=-1, keepdims=True)
    d = o - mu
    var = jnp.mean(d * d, axis=-1, keepdims=True)
    return d * lax.rsqrt(var + NORM_EPS) * g_norm * _silu(gate)


def _retention_kernel(q_ref, k_ref, v_ref, g_ref, cos_ref, sin_ref, decay_ref, qdec_ref, kdec_ref,
                      gn_ref, gstate_ref, r_ref, s_ref):
    @pl.when(pl.program_id(1) == 0)
    def _():
        s_ref[...] = jnp.zeros_like(s_ref)

    c = cos_ref[...]
    s = sin_ref[...]
    for h in range(N_RET_HEADS):
        sl = slice(h * HEAD_DIM, (h + 1) * HEAD_DIM)
        q = _rotate(q_ref[:, sl], c, s)
        k = _rotate(k_ref[:, sl], c, s) * (HEAD_DIM ** -0.5)
        qb = q.astype(BF16)
        kb = k.astype(BF16)
        vb = v_ref[:, sl].astype(BF16)
        s_prev = s_ref[h]
        scores = lax.dot_general(qb, kb, (((1,), (1,)), ((), ())), preferred_element_type=F32)
        scores = scores * decay_ref[h]
        inner = jnp.dot(scores.astype(BF16), vb, preferred_element_type=F32)
        cross = jnp.dot(qb, s_prev.astype(BF16), preferred_element_type=F32) * qdec_ref[:, sl]
        kd_t = (k * kdec_ref[:, sl]).T.astype(BF16)
        s_ref[h] = gstate_ref[h] * s_prev + jnp.dot(kd_t, vb, preferred_element_type=F32)
        r_ref[:, sl] = _group_norm_gate(inner + cross, gn_ref[:, sl], g_ref[:, sl]).astype(r_ref.dtype)


def _retention_prompt(z, batch, seq, tabs, ret_g):
    nc = seq // RET_CHUNK
    wblk = D_RET

    def zspec(off):
        return pl.BlockSpec((RET_CHUNK, wblk), lambda b, c, o=off // wblk: (b * nc + c, o))

    full3 = pl.BlockSpec((N_RET_HEADS, HEAD_DIM, HEAD_DIM), lambda b, c: (0, 0, 0))
    row = pl.BlockSpec((RET_CHUNK, wblk), lambda b, c: (0, 0))
    return pl.pallas_call(
        _retention_kernel,
        grid=(batch, nc),
        in_specs=[zspec(OFF_QR), zspec(OFF_KR), zspec(OFF_VR), zspec(OFF_GR),
                  pl.BlockSpec((RET_CHUNK, HEAD_DIM), lambda b, c: (c, 0)),
                  pl.BlockSpec((RET_CHUNK, HEAD_DIM), lambda b, c: (c, 0)),
                  full3, row, row,
                  pl.BlockSpec((1, wblk), lambda b, c: (0, 0)),
                  full3],
        out_specs=[pl.BlockSpec((RET_CHUNK, wblk), lambda b, c: (b * nc + c, 0)),
                   pl.BlockSpec((None, N_RET_HEADS, HEAD_DIM, HEAD_DIM), lambda b, c: (b, 0, 0, 0))],
        out_shape=[jax.ShapeDtypeStruct((batch * seq, wblk), BF16),
                   jax.ShapeDtypeStruct((batch, N_RET_HEADS, HEAD_DIM, HEAD_DIM), F32)],
        compiler_params=_cparams(("parallel", "arbitrary")),
        name="retention_prompt",
    )(z, z, z, z, tabs["cos"], tabs["sin"], tabs["decay"], tabs["qdec"], tabs["kdec"],
      ret_g.reshape(1, wblk), tabs["gstate"])


def _dsa_prompt_kernel(qa_ref, ga0_ref, ga1_ref, qi0_ref, qi1_ref, qi2_ref, qi3_ref, wi_ref,
                       ki_ref, ka_ref, va_ref, a_ref,
                       kib, kab, vab, qall, qc, sc, m_ref, l_ref, acc_ref, *, n_sel):
    j = pl.program_id(1)
    nkc_all = ki_ref.shape[0] // KEY_CHUNK

    @pl.when(j == 0)
    def _():
        for kc in range(nkc_all):
            rows = slice(kc * KEY_CHUNK, (kc + 1) * KEY_CHUNK)
            kib[kc] = ki_ref[rows, :].astype(BF16)
            for c in range(N_KV_HEADS):
                cols = slice(c * HEAD_DIM, (c + 1) * HEAD_DIM)
                kab[c, kc] = ka_ref[rows, cols].astype(BF16)
                vab[c, kc] = va_ref[rows, cols].astype(BF16)

    nch = (j * Q_BLOCK + Q_BLOCK + KEY_CHUNK - 1) // KEY_CHUNK

    qi_refs = (qi0_ref, qi1_ref, qi2_ref, qi3_ref)
    per = N_IDX_HEADS // len(qi_refs)
    for h in range(N_IDX_HEADS):
        src = qi_refs[h // per]
        qall[h * Q_BLOCK:(h + 1) * Q_BLOCK, :] = src[:, (h % per) * D_IDX:(h % per + 1) * D_IDX].astype(BF16)
    scale = HEAD_DIM ** -0.5
    for c in range(N_KV_HEADS):
        for g in range(N_GROUP):
            hh = c * N_GROUP + g
            qc[c, g * Q_BLOCK:(g + 1) * Q_BLOCK, :] = (
                qa_ref[:, hh * HEAD_DIM:(hh + 1) * HEAD_DIM] * scale).astype(BF16)

    w_norm = N_IDX_HEADS ** -0.5 * D_IDX ** -0.5
    wcols = [wi_ref[:, h:h + 1] * w_norm for h in range(N_IDX_HEADS)]
    t_pos = j * Q_BLOCK + lax.broadcasted_iota(I32, (Q_BLOCK, KEY_CHUNK), 0)
    s_loc = lax.broadcasted_iota(I32, (Q_BLOCK, KEY_CHUNK), 1)

    def idx_body(kc, carry):
        res = lax.dot_general(qall[...], kib[kc], (((1,), (1,)), ((), ())), preferred_element_type=F32)
        acc = jnp.zeros((Q_BLOCK, KEY_CHUNK), F32)
        for h in range(N_IDX_HEADS):
            acc = acc + wcols[h] * jnp.maximum(res[h * Q_BLOCK:(h + 1) * Q_BLOCK, :], 0.0)
        causal = (kc * KEY_CHUNK + s_loc) <= t_pos
        sc[kc] = jnp.where(causal, acc, -jnp.inf)
        return carry

    lax.fori_loop(0, nch, idx_body, 0)

    def count_ge(cand):
        def cnt_body(kc, cnt):
            return cnt + jnp.where(sc[kc] >= cand, 1.0, 0.0)

        cnt = lax.fori_loop(0, nch, cnt_body, jnp.zeros((Q_BLOCK, KEY_CHUNK), F32))
        return jnp.sum(cnt, axis=1, keepdims=True)

    thr = _bisect_threshold(count_ge, (Q_BLOCK, 1), n_sel)

    gates = (ga0_ref, ga1_ref)
    for c in range(N_KV_HEADS):
        m_ref[...] = jnp.full(m_ref.shape, NEG_BIG, F32)
        l_ref[...] = jnp.zeros(l_ref.shape, F32)
        acc_ref[...] = jnp.zeros(acc_ref.shape, F32)

        def att_body(kc, carry, c=c):
            lg = lax.dot_general(qc[c], kab[c, kc], (((1,), (1,)), ((), ())), preferred_element_type=F32)
            bias = jnp.where(sc[kc] >= thr, 0.0, NEG_BIG)
            vch = vab[c, kc]
            for g in range(N_GROUP):
                rows = slice(g * Q_BLOCK, (g + 1) * Q_BLOCK)
                lgg = lg[rows, :] + bias
                m_prev = m_ref[rows, :]
                m_new = jnp.maximum(m_prev, jnp.max(lgg, axis=1, keepdims=True))
                alpha = jnp.exp(m_prev - m_new)
                p = jnp.exp(lgg - m_new[:, :1])
                l_ref[rows, :] = alpha * l_ref[rows, :] + jnp.sum(p, axis=1, keepdims=True)
                acc_ref[rows, :] = alpha * acc_ref[rows, :] + jnp.dot(
                    p.astype(BF16), vch, preferred_element_type=F32)
                m_ref[rows, :] = m_new
            return carry

        lax.fori_loop(0, nch, att_body, 0)

        for g in range(N_GROUP):
            rows = slice(g * Q_BLOCK, (g + 1) * Q_BLOCK)
            o = acc_ref[rows, :] / l_ref[rows, :]
            gate = gates[c][:, g * HEAD_DIM:(g + 1) * HEAD_DIM]
            hh = c * N_GROUP + g
            a_ref[:, hh * HEAD_DIM:(hh + 1) * HEAD_DIM] = (o * _silu(gate)).astype(a_ref.dtype)


def _dsa_prompt(z, batch, seq):
    nq = seq // Q_BLOCK
    n_sel = min(TOPK_MAX, seq // 4)
    nkc = seq // KEY_CHUNK

    def qspec(off, width):
        return pl.BlockSpec((Q_BLOCK, width), lambda b, j, o=off // width: (b * nq + j, o))

    def kspec(off, width):
        return pl.BlockSpec((seq, width), lambda b, j, o=off // width: (b, o))

    half = D_ATT // 2
    quarter = N_IDX_HEADS * D_IDX // 4
    kernel = functools.partial(_dsa_prompt_kernel, n_sel=n_sel)
    return pl.pallas_call(
        kernel,
        grid=(batch, nq),
        in_specs=[qspec(OFF_QA, D_ATT),
                  qspec(OFF_GA, half), qspec(OFF_GA + half, half),
                  qspec(OFF_QI, quarter), qspec(OFF_QI + quarter, quarter),
                  qspec(OFF_QI + 2 * quarter, quarter), qspec(OFF_QI + 3 * quarter, quarter),
                  qspec(OFF_WI, 128),
                  kspec(OFF_KI, D_IDX), kspec(OFF_KA, D_KV), kspec(OFF_VA, D_KV)],
        out_specs=pl.BlockSpec((Q_BLOCK, D_ATT), lambda b, j: (b * nq + j, 0)),
        out_shape=jax.ShapeDtypeStruct((batch * seq, D_ATT), BF16),
        scratch_shapes=[pltpu.VMEM((nkc, KEY_CHUNK, D_IDX), BF16),
                        pltpu.VMEM((N_KV_HEADS, nkc, KEY_CHUNK, HEAD_DIM), BF16),
                        pltpu.VMEM((N_KV_HEADS, nkc, KEY_CHUNK, HEAD_DIM), BF16),
                        pltpu.VMEM((N_IDX_HEADS * Q_BLOCK, D_IDX), BF16),
                        pltpu.VMEM((N_KV_HEADS, N_GROUP * Q_BLOCK, HEAD_DIM), BF16),
                        pltpu.VMEM((nkc, Q_BLOCK, KEY_CHUNK), F32),
                        pltpu.VMEM((N_GROUP * Q_BLOCK, HEAD_DIM), F32),
                        pltpu.VMEM((N_GROUP * Q_BLOCK, HEAD_DIM), F32),
                        pltpu.VMEM((N_GROUP * Q_BLOCK, HEAD_DIM), F32)],
        compiler_params=_cparams(("parallel", "arbitrary")),
        name="dsa_prompt",
    )(z, z, z, z, z, z, z, z, z, z, z)


def _outproj_kernel(r_ref, a_ref, x_ref, w_ref, g_ref, y_ref):
    m = jnp.dot(r_ref[...], w_ref[:D_RET, :], preferred_element_type=F32)
    m = m + jnp.dot(a_ref[...], w_ref[D_RET:, :], preferred_element_type=F32)
    hp = x_ref[...] + m
    ms = jnp.mean(hp * hp, axis=-1, keepdims=True)
    y_ref[...] = hp * lax.rsqrt(ms + NORM_EPS) * g_ref[...]


def _out_projection(r, a, x2d, w_out_bf16, final_g, tm=512):
    n, d = x2d.shape
    return pl.pallas_call(
        _outproj_kernel,
        grid=(n // tm,),
        in_specs=[pl.BlockSpec((tm, D_RET), lambda i: (i, 0)),
                  pl.BlockSpec((tm, D_ATT), lambda i: (i, 0)),
                  pl.BlockSpec((tm, d), lambda i: (i, 0)),
                  pl.BlockSpec((D_RET + D_ATT, d), lambda i: (0, 0)),
                  pl.BlockSpec((1, d), lambda i: (0, 0))],
        out_specs=pl.BlockSpec((tm, d), lambda i: (i, 0)),
        out_shape=jax.ShapeDtypeStruct((n, d), F32),
        compiler_params=_cparams(("parallel",)),
        name="out_projection",
    )(r, a, x2d, w_out_bf16, final_g.reshape(1, d))


PAGES_PER_STEP = 8


def _sample_scores_kernel(pt_ref, qi_ref, wi_ref, *refs):
    page_refs = refs[:PAGES_PER_STEP]
    out_ref = refs[PAGES_PER_STEP]
    qb = qi_ref[...].astype(BF16)
    w = wi_ref[...] * (N_IDX_HEADS ** -0.5 * D_IDX ** -0.5)
    rows = []
    for p in range(PAGES_PER_STEP):
        kp = page_refs[p][...].astype(BF16)
        rel = lax.dot_general(qb, kp, (((1,), (1,)), ((), ())), preferred_element_type=F32)
        rows.append(jnp.sum(w * jnp.maximum(rel, 0.0), axis=0, keepdims=True))
    out_ref[...] = jnp.concatenate(rows, axis=0)


def _sample_scores(qi_s, wi_s, cache_kidx, layer, page_table):
    db, n_pages = page_table.shape
    page = cache_kidx.shape[2]
    groups = n_pages // PAGES_PER_STEP

    def pspec(p):
        return pl.BlockSpec((None, None, page, D_IDX),
                            lambda b, g, pt, p=p: (layer, pt[b, g * PAGES_PER_STEP + p], 0, 0))

    grid_spec = pltpu.PrefetchScalarGridSpec(
        num_scalar_prefetch=1,
        grid=(db, groups),
        in_specs=[pl.BlockSpec((None, N_IDX_HEADS, D_IDX), lambda b, g, pt: (b, 0, 0)),
                  pl.BlockSpec((None, N_IDX_HEADS, 1), lambda b, g, pt: (b, 0, 0))]
                 + [pspec(p) for p in range(PAGES_PER_STEP)],
        out_specs=pl.BlockSpec((None, PAGES_PER_STEP, page), lambda b, g, pt: (b, g, 0)),
    )
    return pl.pallas_call(
        _sample_scores_kernel,
        grid_spec=grid_spec,
        out_shape=jax.ShapeDtypeStruct((db, n_pages, page), F32),
        compiler_params=_cparams(("parallel", "arbitrary")),
        name="sample_scores",
    )(page_table, qi_s, wi_s, *([cache_kidx] * PAGES_PER_STEP))


def _sample_threshold_kernel(sc_ref, qi_ref, wi_ref, kin_ref, thr_ref, snew_sc_ref, *, n_sel):
    prod = qi_ref[...] * jnp.concatenate([kin_ref[...]] * N_IDX_HEADS, axis=1)
    w = wi_ref[...] * (N_IDX_HEADS ** -0.5 * D_IDX ** -0.5)
    s_new = jnp.zeros((prod.shape[0], 1), F32)
    for h in range(N_IDX_HEADS):
        rel = jnp.sum(prod[:, h * D_IDX:(h + 1) * D_IDX], axis=1, keepdims=True)
        s_new = s_new + w[:, h:h + 1] * jnp.maximum(rel, 0.0)
    past_scores = sc_ref[...]

    def count_ge(cand):
        cnt = jnp.sum(jnp.where(past_scores >= cand, 1.0, 0.0), axis=1, keepdims=True)
        return cnt + jnp.where(s_new >= cand, 1.0, 0.0)

    thr_ref[...] = _bisect_threshold(count_ge, s_new.shape, n_sel)
    snew_sc_ref[...] = s_new


def _sample_threshold(scores2d, qi_s, wi_s, ki_new, n_sel):
    db = scores2d.shape[0]
    kernel = functools.partial(_sample_threshold_kernel, n_sel=n_sel)
    return pl.pallas_call(
        kernel,
        out_shape=[jax.ShapeDtypeStruct((db, 1), F32), jax.ShapeDtypeStruct((db, 1), F32)],
        compiler_params=pltpu.CompilerParams(vmem_limit_bytes=V7X_VMEM_LIMIT),
        name="sample_threshold",
    )(scores2d, qi_s, wi_s, ki_new)


def _sample_attend_kernel(pt_ref, sc_ref, thr_ref, snew_sc_ref, q_ref, kn_ref, vn_ref, ga_ref, *refs):
    kp_refs = refs[:PAGES_PER_STEP]
    vp_refs = refs[PAGES_PER_STEP:2 * PAGES_PER_STEP]
    o_ref = refs[2 * PAGES_PER_STEP]
    m_ref, l_ref, acc_ref = refs[2 * PAGES_PER_STEP + 1:]
    g = pl.program_id(1)

    @pl.when(g == 0)
    def _():
        m_ref[...] = jnp.full(m_ref.shape, NEG_BIG, F32)
        l_ref[...] = jnp.zeros(l_ref.shape, F32)
        acc_ref[...] = jnp.zeros(acc_ref.shape, F32)

    thr = thr_ref[...]
    q = q_ref[...]
    qb = q.astype(BF16)
    kv_of_head = lax.broadcasted_iota(I32, (N_ATT_HEADS, 1), 0) // N_GROUP

    def per_kv_head(fn):
        out = fn(0)
        for c in range(1, N_KV_HEADS):
            out = jnp.where(kv_of_head == c, fn(c), out)
        return out

    def logits(c):
        kc = jnp.concatenate([r[:, c, :] for r in kp_refs], axis=0).astype(BF16)
        return lax.dot_general(qb, kc, (((1,), (1,)), ((), ())), preferred_element_type=F32)

    lg = per_kv_head(logits)
    step_scores = jnp.concatenate([sc_ref[p:p + 1, :] for p in range(PAGES_PER_STEP)], axis=1)
    lg = lg + jnp.where(step_scores >= thr, 0.0, NEG_BIG)
    m_prev = m_ref[...]
    m_new = jnp.maximum(m_prev, jnp.max(lg, axis=1, keepdims=True))
    alpha = jnp.exp(m_prev - m_new)
    p = jnp.exp(lg - m_new[:, :1])
    l_ref[...] = alpha * l_ref[...] + jnp.sum(p, axis=1, keepdims=True)
    pb = p.astype(BF16)

    def weighted_values(c):
        vc = jnp.concatenate([r[:, c, :] for r in vp_refs], axis=0).astype(BF16)
        return jnp.dot(pb, vc, preferred_element_type=F32)

    acc_ref[...] = alpha * acc_ref[...] + per_kv_head(weighted_values)
    m_ref[...] = m_new

    @pl.when(g == pl.num_programs(1) - 1)
    def _():
        k_n = per_kv_head(lambda c: kn_ref[c:c + 1, :])
        v_n = per_kv_head(lambda c: vn_ref[c:c + 1, :])
        lg_n = jnp.sum(q * k_n, axis=1, keepdims=True)
        lg_n = lg_n + jnp.where(snew_sc_ref[...] >= thr, 0.0, NEG_BIG)
        m_p = m_ref[...]
        m_n = jnp.maximum(m_p, lg_n)
        al = jnp.exp(m_p - m_n)
        p_n = jnp.exp(lg_n - m_n[:, :1])
        l_fin = al * l_ref[...] + p_n
        acc = al * acc_ref[...] + p_n * v_n
        o_ref[...] = acc / l_fin * _silu(ga_ref[...])


def _sample_attend(scores, thr, score_new, q_s, kn, vn, ga_s, cache_k, cache_v, layer, page_table):
    db, n_pages = page_table.shape
    page = cache_k.shape[2]
    groups = n_pages // PAGES_PER_STEP

    def pspec(p):
        return pl.BlockSpec((None, None, page, N_KV_HEADS, HEAD_DIM),
                            lambda b, g, pt, p=p: (layer, pt[b, g * PAGES_PER_STEP + p], 0, 0, 0))

    def bspec(shape):
        return pl.BlockSpec((None,) + shape, lambda b, g, pt: (b,) + (0,) * len(shape))

    hd = (N_ATT_HEADS, HEAD_DIM)
    grid_spec = pltpu.PrefetchScalarGridSpec(
        num_scalar_prefetch=1,
        grid=(db, groups),
        in_specs=[pl.BlockSpec((None, PAGES_PER_STEP, page), lambda b, g, pt: (b, g, 0)),
                  bspec((1, 1)), bspec((1, 1)),
                  bspec(hd), bspec((N_KV_HEADS, HEAD_DIM)), bspec((N_KV_HEADS, HEAD_DIM)), bspec(hd)]
                 + [pspec(p) for p in range(PAGES_PER_STEP)] * 2,
        out_specs=bspec(hd),
        scratch_shapes=[pltpu.VMEM(hd, F32), pltpu.VMEM(hd, F32), pltpu.VMEM(hd, F32)],
    )
    return pl.pallas_call(
        _sample_attend_kernel,
        grid_spec=grid_spec,
        out_shape=jax.ShapeDtypeStruct((db,) + hd, F32),
        compiler_params=_cparams(("parallel", "arbitrary")),
        name="sample_attend",
    )(page_table, scores, thr.reshape(db, 1, 1), score_new.reshape(db, 1, 1), q_s, kn, vn, ga_s,
      *([cache_k] * PAGES_PER_STEP), *([cache_v] * PAGES_PER_STEP))


def _sample_retention_kernel(q_ref, k_ref, v_ref, g_ref, cos_ref, sin_ref, gn_ref, gam_ref,
                             st_ref, r_ref, snew_ref):
    c = cos_ref[...]
    s = sin_ref[...]
    q = _rotate(q_ref[...], c, s)
    k = _rotate(k_ref[...], c, s) * (HEAD_DIM ** -0.5)
    v = v_ref[...]
    inner = jnp.sum(q * k, axis=1, keepdims=True) * v
    k_t = jnp.concatenate([k, jnp.zeros((HEAD_DIM - N_RET_HEADS, HEAD_DIM), F32)], axis=0).T
    outs = []
    for h in range(N_RET_HEADS):
        s_prev = st_ref[h]
        gam = gam_ref[h]
        q8 = jnp.broadcast_to(q[h:h + 1, :], (8, HEAD_DIM)).astype(BF16)
        cross = jnp.dot(q8, s_prev.astype(BF16), preferred_element_type=F32)[:1, :] * gam[:1, :]
        outs.append(inner[h:h + 1, :] + cross)
        snew_ref[h] = gam * s_prev + k_t[:, h:h + 1] * v[h:h + 1, :]
    o = jnp.concatenate(outs, axis=0)
    r_ref[...] = _group_norm_gate(o, gn_ref[...], g_ref[...])


def _sample_retention(qr, kr, vr, gr, cos_s, sin_s, ret_g, gam_tab, state):
    db = qr.shape[0]

    def bspec(shape):
        return pl.BlockSpec((None,) + shape, lambda b: (b,) + (0,) * len(shape))

    def cspec(shape):
        return pl.BlockSpec(shape, lambda b: (0,) * len(shape))

    hd = (N_RET_HEADS, HEAD_DIM)
    st = (N_RET_HEADS, HEAD_DIM, HEAD_DIM)
    return pl.pallas_call(
        _sample_retention_kernel,
        grid=(db,),
        in_specs=[bspec(hd), bspec(hd), bspec(hd), bspec(hd),
                  cspec((1, HEAD_DIM)), cspec((1, HEAD_DIM)), cspec(hd), cspec(st), bspec(st)],
        out_specs=[bspec(hd), bspec(st)],
        out_shape=[jax.ShapeDtypeStruct((db,) + hd, F32), jax.ShapeDtypeStruct((db,) + st, F32)],
        compiler_params=_cparams(("parallel",)),
        name="sample_retention",
    )(qr, kr, vr, gr, cos_s, sin_s, ret_g, gam_tab, state)


def _sample_out_kernel(r_ref, a_ref, x_ref, w_ref, g_ref, y_ref):
    m = jnp.dot(r_ref[...].astype(BF16), w_ref[:D_RET, :], preferred_element_type=F32)
    m = m + jnp.dot(a_ref[...].astype(BF16), w_ref[D_RET:, :], preferred_element_type=F32)
    hp = x_ref[...] + m
    ms = jnp.mean(hp * hp, axis=-1, keepdims=True)
    y_ref[...] = hp * lax.rsqrt(ms + NORM_EPS) * g_ref[...]


def _sample_out(r_s, a_s, x_s, w_out_bf16, final_g):
    n, d = x_s.shape
    return pl.pallas_call(
        _sample_out_kernel,
        out_shape=jax.ShapeDtypeStruct((n, d), F32),
        compiler_params=pltpu.CompilerParams(vmem_limit_bytes=V7X_VMEM_LIMIT),
        name="sample_out",
    )(r_s, a_s, x_s, w_out_bf16, final_g.reshape(1, d))


def _rotary_tables(pos):
    half = HEAD_DIM // 2
    inv = ROPE_BASE ** (-jnp.arange(half, dtype=F32) / half)
    ang = pos.astype(F32)[:, None] * inv[None, :]
    cos, sin = jnp.cos(ang), jnp.sin(ang)
    return jnp.concatenate([cos, cos], -1), jnp.concatenate([-sin, sin], -1)


def _retention_tables(seq):
    log_g = jnp.log1p(-jnp.exp2(-5.0 - jnp.arange(N_RET_HEADS, dtype=F32)))
    c = RET_CHUNK
    i = jnp.arange(c, dtype=F32)
    diff = i[:, None] - i[None, :]
    decay = jnp.where(diff[None] >= 0, jnp.exp(jnp.maximum(diff, 0.0)[None] * log_g[:, None, None]), 0.0)
    q_dec = jnp.exp((i[:, None] + 1.0) * log_g[None, :])
    k_dec = jnp.exp((c - 1.0 - i)[:, None] * log_g[None, :])
    ones = jnp.ones((1, 1, HEAD_DIM), F32)
    cos, sin = _rotary_tables(jnp.arange(seq, dtype=I32))
    return {
        "decay": decay,
        "qdec": (q_dec[:, :, None] * ones).reshape(c, D_RET),
        "kdec": (k_dec[:, :, None] * ones).reshape(c, D_RET),
        "gstate": jnp.exp(c * log_g)[:, None, None] * jnp.ones((1, HEAD_DIM, HEAD_DIM), F32),
        "gam1": jnp.exp(log_g)[:, None, None] * jnp.ones((1, HEAD_DIM, HEAD_DIM), F32),
        "cos": cos, "sin": sin,
    }


def kernel(x_prompt, x_sample, cache_k, cache_v, cache_kidx, state_ret, page_table,
           norm_g, w_in, ret_norm_g, w_out, final_norm_g):
    depth = w_in.shape[0]
    batch, seq, d_model = x_prompt.shape
    db, ds, _ = x_sample.shape
    assert depth == 1 and ds == 1 and seq % KEY_CHUNK == 0 and w_in.shape[2] == D_IN_PROJ
    page = cache_k.shape[2]
    past = page_table.shape[1] * page
    n_sel_s = min(TOPK_MAX, (past + ds) // 4)

    tabs = _retention_tables(seq)
    cos_s, sin_s = _rotary_tables(past + jnp.arange(ds, dtype=I32))

    hp = x_prompt.reshape(batch * seq, d_model)
    hs = x_sample.reshape(db * ds, d_model)
    pk, pv, pki, pst, sk, sv, ski, sst = [], [], [], [], [], [], [], []
    for l in range(depth):
        h_b = _rmsnorm_bf16(hp, norm_g[l], 512)
        hs_b = _rmsnorm_bf16(hs, norm_g[l], db * ds)
        z, zs = _in_projection(h_b, hs_b, w_in[l].T)
        w_out_b = w_out[l].astype(BF16)
        fin_g = final_norm_g

        r, s_fin = _retention_prompt(z, batch, seq, tabs, ret_norm_g[l])
        a = _dsa_prompt(z, batch, seq)
        hp = _out_projection(r, a, hp, w_out_b, fin_g)
        pk.append(z[:, OFF_KA:OFF_VA].reshape(batch, seq, N_KV_HEADS, HEAD_DIM))
        pv.append(z[:, OFF_VA:OFF_GA].reshape(batch, seq, N_KV_HEADS, HEAD_DIM))
        pki.append(z[:, OFF_KI:OFF_WI].reshape(batch, seq, D_IDX))
        pst.append(s_fin)

        def seg(off, width):
            return zs[:, off:off + width]

        qi_s = seg(OFF_QI, N_IDX_HEADS * D_IDX).reshape(db, N_IDX_HEADS, D_IDX)
        wi_s = seg(OFF_WI, N_IDX_HEADS).reshape(db, N_IDX_HEADS, 1)
        ki_s = seg(OFF_KI, D_IDX)
        ka_s = seg(OFF_KA, D_KV)
        va_s = seg(OFF_VA, D_KV)
        scores = _sample_scores(qi_s, wi_s, cache_kidx, l, page_table)
        thr, score_new = _sample_threshold(scores.reshape(db, past), seg(OFF_QI, N_IDX_HEADS * D_IDX),
                                           seg(OFF_WI, N_IDX_HEADS), ki_s, n_sel_s)
        qa_s = seg(OFF_QA, D_ATT).reshape(db, N_ATT_HEADS, HEAD_DIM) * (HEAD_DIM ** -0.5)
        a_s = _sample_attend(scores, thr, score_new, qa_s, ka_s.reshape(db, N_KV_HEADS, HEAD_DIM),
                             va_s.reshape(db, N_KV_HEADS, HEAD_DIM),
                             seg(OFF_GA, D_ATT).reshape(db, N_ATT_HEADS, HEAD_DIM),
                             cache_k, cache_v, l, page_table)

        hd = (db, N_RET_HEADS, HEAD_DIM)
        r_s, s_new = _sample_retention(seg(OFF_QR, D_RET).reshape(hd), seg(OFF_KR, D_RET).reshape(hd),
                                       seg(OFF_VR, D_RET).reshape(hd), seg(OFF_GR, D_RET).reshape(hd),
                                       cos_s, sin_s, ret_norm_g[l], tabs["gam1"], state_ret[l])
        hs = _sample_out(r_s.reshape(db, D_RET), a_s.reshape(db, D_ATT), hs, w_out_b, fin_g)
        sk.append(ka_s.reshape(db, ds, N_KV_HEADS, HEAD_DIM))
        sv.append(va_s.reshape(db, ds, N_KV_HEADS, HEAD_DIM))
        ski.append(ki_s.reshape(db, ds, D_IDX))
        sst.append(s_new)

    y_prompt = hp.reshape(batch, seq, d_model)
    y_sample = hs.reshape(db, ds, d_model)
    return (y_prompt, y_sample, jnp.stack(pk), jnp.stack(pv), jnp.stack(pki), jnp.stack(pst),
            jnp.stack(sk), jnp.stack(sv), jnp.stack(ski), jnp.stack(sst))
```
